```python
import math
import jax, jax.numpy as jnp
from jax import lax
import numpy as np

D_MODEL = 1024
BATCH = 8
SEQ = 2048
DEPTH = 4

N_MIXERS = 2
N_RET_LAYERS = (DEPTH + 1) // 2
N_MLA_LAYERS = DEPTH // 2
MIX_WIDTH = D_MODEL
MEM_LEN = 256
MEM_HEADS = 4
MEM_HEAD_DIM = 64
MEM_WIDTH = MEM_HEADS * MEM_HEAD_DIM
TOK_WIDTH = MIX_WIDTH - MEM_WIDTH
RET_HEADS = 6
RET_DK = 128
RET_DV = TOK_WIDTH // RET_HEADS
RET_CHUNK = 128
RET_IN = 4 * TOK_WIDTH + MEM_WIDTH
MLA_HEADS = 6
MLA_NOPE = 64
MLA_ROPE = 32
MLA_DV = TOK_WIDTH // MLA_HEADS
MLA_Q_RANK = 256
MLA_KV_RANK = 128
MLA_IN = MLA_Q_RANK + MLA_KV_RANK + MLA_ROPE + MEM_WIDTH
MLA_SCALE = (MLA_NOPE + MLA_ROPE) ** -0.5
Q_BLOCK = 128
D_FF = 2816
FFN_RES_WEIGHT = 0.5
ROPE_THETA = 10000.0
NORM_EPS = 1e-6

kernel_name = "hybrid_retention_mla_memory_macaron_encoder"


def rms_norm(x, g):
    xf = x.astype(jnp.float32)
    y = xf * lax.rsqrt(jnp.mean(xf * xf, axis=-1, keepdims=True) + NORM_EPS)
    return (y * g.astype(jnp.float32)).astype(x.dtype)


def head_layer_norm(y, g):
    mu = jnp.mean(y, axis=-1, keepdims=True)
    var = jnp.mean(jnp.square(y - mu), axis=-1, keepdims=True)
    return (y - mu) * lax.rsqrt(var + NORM_EPS) * g.astype(jnp.float32)


def rope_tables(positions, dim):
    inv = 1.0 / (ROPE_THETA ** (jnp.arange(0, dim, 2, dtype=jnp.float32) / dim))
    ang = positions.astype(jnp.float32)[..., None] * inv
    return jnp.cos(ang), jnp.sin(ang)


def apply_rope(x, cos, sin):
    half = x.shape[-1] // 2
    x1, x2 = x[..., :half], x[..., half:]
    c = cos[:, :, None, :].astype(x.dtype)
    s = sin[:, :, None, :].astype(x.dtype)
    return jnp.concatenate([x1 * c - x2 * s, x1 * s + x2 * c], axis=-1)


def swiglu(h, w_gu, w_down):
    g, u = jnp.split(h @ w_gu, 2, axis=-1)
    return (jax.nn.silu(g) * u) @ w_down


def retention_chunkwise(q, k, v, log_gamma, strict):
    B, S, H, dk = q.shape
    dv = v.shape[-1]
    n = S // RET_CHUNK
    C = RET_CHUNK
    q = q.astype(jnp.float32).reshape(B, n, C, H, dk)
    k = k.astype(jnp.float32).reshape(B, n, C, H, dk)
    v = v.astype(jnp.float32).reshape(B, n, C, H, dv)
    lg = log_gamma.astype(jnp.float32)
    idx = jnp.arange(C, dtype=jnp.float32)
    diff = idx[:, None] - idx[None, :]
    mask = diff > 0 if strict else diff >= 0
    decay = jnp.where(mask[None], jnp.exp(jnp.where(mask, diff, 0.0)[None] * lg[:, None, None]), 0.0)
    scores = jnp.einsum('bnihd,bnjhd->bnhij', q, k) * decay[None, None]
    intra = jnp.einsum('bnhij,bnjhv->bnihv', scores, v)
    k_w = k * jnp.exp((C - 1.0 - idx)[:, None] * lg[None, :])[None, None, :, :, None]
    chunk_kv = jnp.einsum('bnjhd,bnjhv->bnhdv', k_w, v)
    g_chunk = jnp.exp(C * lg)[None, :, None, None]

    def step(state, kv_n):
        return state * g_chunk + kv_n, state

    init = jnp.zeros((B, H, dk, dv), jnp.float32)
    _, states = lax.scan(step, init, jnp.moveaxis(chunk_kv, 1, 0))
    states = jnp.moveaxis(states, 0, 1)
    q_w = q * jnp.exp((idx + 1.0)[:, None] * lg[None, :])[None, None, :, :, None]
    cross = jnp.einsum('bnihd,bnhdv->bnihv', q_w, states)
    return (intra + cross).reshape(B, S, H, dv)


def retention_mixer(h, w_in, log_decay, head_norm, cos, sin):
    B, S, _ = h.shape
    T = TOK_WIDTH
    q, k, v, g, qm = jnp.split(h @ w_in, [T, 2 * T, 3 * T, 4 * T], axis=-1)
    q = apply_rope(q.reshape(B, S, RET_HEADS, RET_DK), cos, sin)
    k = apply_rope(k.reshape(B, S, RET_HEADS, RET_DK), cos, sin) * (RET_DK ** -0.5)
    v = v.reshape(B, S, RET_HEADS, RET_DV)
    y_fwd = retention_chunkwise(q, k, v, log_decay[0], strict=False)
    y_bwd = retention_chunkwise(q[:, ::-1], k[:, ::-1], v[:, ::-1], log_decay[1], strict=True)[:, ::-1]
    y = head_layer_norm(y_fwd + y_bwd, head_norm).reshape(B, S, T).astype(h.dtype)
    return jax.nn.silu(g) * y, qm


def dense_attention_blocked(q, k, v, scale):
    B, S, H, dq = q.shape
    dv = v.shape[-1]
    n = S // Q_BLOCK
    qb = q.reshape(B, n, Q_BLOCK, H, dq).transpose(1, 0, 2, 3, 4)

    def one_block(q_blk):
        s = jnp.einsum('bqhd,bkhd->bhqk', q_blk, k).astype(jnp.float32) * scale
        p = jax.nn.softmax(s, axis=-1).astype(v.dtype)
        return jnp.einsum('bhqk,bkhv->bqhv', p, v)

    o = lax.map(one_block, qb)
    return o.transpose(1, 0, 2, 3, 4).reshape(B, S, H * dv)


def mla_mixer(h, w_in, q_norm, kv_norm, w_uq, w_ukv, cos, sin):
    B, S, _ = h.shape
    c_q, c_kv, k_r, qm = jnp.split(
        h @ w_in, [MLA_Q_RANK, MLA_Q_RANK + MLA_KV_RANK, MLA_Q_RANK + MLA_KV_RANK + MLA_ROPE], axis=-1)
    c_q = rms_norm(c_q, q_norm)
    c_kv = rms_norm(c_kv, kv_norm)
    q = (c_q @ w_uq).reshape(B, S, MLA_HEADS, MLA_NOPE + MLA_ROPE)
    q = jnp.concatenate([q[..., :MLA_NOPE], apply_rope(q[..., MLA_NOPE:], cos, sin)], axis=-1)
    kv = (c_kv @ w_ukv).reshape(B, S, MLA_HEADS, MLA_NOPE + MLA_DV)
    k_nope, v = kv[..., :MLA_NOPE], kv[..., MLA_NOPE:]
    k_rope = apply_rope(k_r.reshape(B, S, 1, MLA_ROPE), cos, sin)
    k = jnp.concatenate([k_nope, jnp.broadcast_to(k_rope, (B, S, MLA_HEADS, MLA_ROPE))], axis=-1)
    return dense_attention_blocked(q, k, v, MLA_SCALE), qm


def memory_attention(qm, mem_n, w_kv):
    B, S, _ = qm.shape
    L = mem_n.shape[1]
    q = qm.reshape(B, S, MEM_HEADS, MEM_HEAD_DIM)
    mk, mv = jnp.split(mem_n @ w_kv, 2, axis=-1)
    mk = mk.reshape(B, L, MEM_HEADS, MEM_HEAD_DIM)
    mv = mv.reshape(B, L, MEM_HEADS, MEM_HEAD_DIM)
    s = jnp.einsum('bshd,bmhd->bhsm', q, mk).astype(jnp.float32) * (MEM_HEAD_DIM ** -0.5)
    p = jax.nn.softmax(s, axis=-1).astype(mv.dtype)
    return jnp.einsum('bhsm,bmhd->bshd', p, mv).reshape(B, S, MEM_WIDTH)


def setup_inputs(seed: int = 0) -> dict:
    key = jax.random.key(seed)
    ks = jax.random.split(key, 17)
    f32 = jnp.float32

    def w(k, shape, fan_in):
        return jax.random.normal(k, shape, f32) * (fan_in ** -0.5)

    def gain(k, shape):
        return 1.0 + 0.02 * jax.random.normal(k, shape, f32)

    x = jax.random.normal(ks[0], (BATCH, SEQ, D_MODEL), f32)
    mem = jax.random.normal(ks[1], (BATCH, MEM_LEN, D_MODEL), f32)
    offsets = jax.random.randint(ks[2], (BATCH, 1), 0, 1024, dtype=jnp.int32)
    positions = (offsets + jnp.arange(SEQ, dtype=jnp.int32)[None, :]).astype(jnp.int32)
    base_decay = jnp.log(1.0 - 2.0 ** (-5.0 - jnp.arange(RET_HEADS, dtype=f32)))
    ret_log_decay = base_decay[None, None, :] * (1.0 + 0.05 * jax.random.normal(ks[10], (N_RET_LAYERS, 2, RET_HEADS), f32))
    return {
        "x": x,
        "mem": mem,
        "positions": positions,
        "norm_gains": gain(ks[3], (DEPTH, 6, D_MODEL)),
        "ffn_w_gu": w(ks[4], (DEPTH, 2, D_MODEL, 2 * D_FF), D_MODEL),
        "ffn_w_down": w(ks[5], (DEPTH, 2, D_FF, D_MODEL), D_FF),
        "w_o": w(ks[6], (DEPTH, MIX_WIDTH, D_MODEL), MIX_WIDTH),
        "mem_norm": gain(ks[7], (D_MODEL,)),
        "mem_w_kv": w(ks[8], (DEPTH, D_MODEL, 2 * MEM_WIDTH), D_MODEL),
        "ret_w_in": w(ks[9], (N_RET_LAYERS, D_MODEL, RET_IN), D_MODEL),
        "ret_log_decay": ret_log_decay,
        "ret_head_norm": gain(ks[11], (N_RET_LAYERS, RET_HEADS, RET_DV)),
        "mla_w_in": w(ks[12], (N_MLA_LAYERS, D_MODEL, MLA_IN), D_MODEL),
        "mla_q_norm": gain(ks[13], (N_MLA_LAYERS, MLA_Q_RANK)),
        "mla_kv_norm": gain(ks[14], (N_MLA_LAYERS, MLA_KV_RANK)),
        "mla_w_uq": w(ks[15], (N_MLA_LAYERS, MLA_Q_RANK, MLA_HEADS * (MLA_NOPE + MLA_ROPE)), MLA_Q_RANK),
        "mla_w_ukv": w(ks[16], (N_MLA_LAYERS, MLA_KV_RANK, MLA_HEADS * (MLA_NOPE + MLA_DV)), MLA_KV_RANK),
    }


def reference(x, mem, positions, norm_gains, ffn_w_gu, ffn_w_down, w_o, mem_norm, mem_w_kv,
              ret_w_in, ret_log_decay, ret_head_norm,
              mla_w_in, mla_q_norm, mla_kv_norm, mla_w_uq, mla_w_ukv):
    cos_ret, sin_ret = rope_tables(positions, RET_DK)
    cos_mla, sin_mla = rope_tables(positions, MLA_ROPE)
    mem_n = rms_norm(mem, mem_norm)

    for l in range(DEPTH):
        ng = norm_gains[l]
        x = x + FFN_RES_WEIGHT * rms_norm(swiglu(rms_norm(x, ng[0]), ffn_w_gu[l, 0], ffn_w_down[l, 0]), ng[1])
        h = rms_norm(x, ng[2])
        r = l // N_MIXERS
        if l % N_MIXERS == 0:
            tok, qm = retention_mixer(h, ret_w_in[r], ret_log_decay[r], ret_head_norm[r], cos_ret, sin_ret)
        else:
            tok, qm = mla_mixer(h, mla_w_in[r], mla_q_norm[r], mla_kv_norm[r], mla_w_uq[r], mla_w_ukv[r],
                                cos_mla, sin_mla)
        mem_out = memory_attention(qm, mem_n, mem_w_kv[l])
        mix = jnp.concatenate([tok, mem_out], axis=-1) @ w_o[l]
        x = x + rms_norm(mix, ng[3])
        x = x + FFN_RES_WEIGHT * rms_norm(swiglu(rms_norm(x, ng[4]), ffn_w_gu[l, 1], ffn_w_down[l, 1]), ng[5])
    return x
```

```python
import functools

import jax
import jax.numpy as jnp
from jax import lax
from jax.experimental import pallas as pl
from jax.experimental.pallas import tpu as pltpu

D_MODEL = 1024
BATCH = 8
SEQ = 2048
DEPTH = 4
N_TOK = BATCH * SEQ
MEM_LEN = 256
MEM_HEADS = 4
MEM_HEAD_DIM = 64
MEM_WIDTH = MEM_HEADS * MEM_HEAD_DIM
TOK_WIDTH = D_MODEL - MEM_WIDTH
HEADS = 6
HEAD_W = 128
RET_DK = 128
RET_IN = 4 * TOK_WIDTH + MEM_WIDTH
MLA_NOPE = 64
MLA_ROPE = 32
MLA_Q_RANK = 256
MLA_KV_RANK = 128
MLA_SCALE = (MLA_NOPE + MLA_ROPE) ** -0.5
D_FF = 2816
FFN_RES_WEIGHT = 0.5
ROPE_THETA = 10000.0
NORM_EPS = 1e-6

V7X_VMEM_BYTES = 64 * 1024 * 1024
VMEM_LIMIT = V7X_VMEM_BYTES - 8 * 1024 * 1024
TM = 512
FF_CHUNK = 256
RET_CHUNK = 256
TQ = 512

F32 = jnp.float32
BF16 = jnp.bfloat16


def _dot(a, b):
    return jnp.dot(a, b, preferred_element_type=F32)


def _dot_nt(a, b):
    return lax.dot_general(a, b, (((1,), (1,)), ((), ())), preferred_element_type=F32)


def _dot_tn(a, b):
    return lax.dot_general(a, b, (((0,), (0,)), ((), ())), preferred_element_type=F32)


def _rms(x, g):
    ms = jnp.mean(x * x, axis=-1, keepdims=True)
    return x * lax.rsqrt(ms + NORM_EPS) * g


def _silu(g):
    return g * jax.nn.sigmoid(g)


def _params(*semantics):
    return pltpu.CompilerParams(dimension_semantics=semantics, vmem_limit_bytes=VMEM_LIMIT)


def _resident(shape):
    return pl.BlockSpec(shape, lambda *_: (0,) * len(shape), pipeline_mode=pl.Buffered(1))


def _rope_body(pos_ref, rows_ref, cos_ref, sin_ref):
    ang = pos_ref[...] * rows_ref[0:1, :]
    cos_ref[...] = jnp.cos(ang) * rows_ref[1:2, :]
    sin_ref[...] = jnp.sin(ang) * rows_ref[2:3, :]


def _rope_tables(pos, rows):
    tm = 2048
    return pl.pallas_call(
        _rope_body,
        grid=(2, N_TOK // tm),
        in_specs=[pl.BlockSpec((tm, 1), lambda t, i: (i, 0)),
                  pl.BlockSpec((None, 3, HEAD_W), lambda t, i: (t, 0, 0))],
        out_specs=[pl.BlockSpec((None, tm, HEAD_W), lambda t, i: (t, i, 0)),
                   pl.BlockSpec((None, tm, HEAD_W), lambda t, i: (t, i, 0))],
        out_shape=[jax.ShapeDtypeStruct((2, N_TOK, HEAD_W), F32)] * 2,
        compiler_params=_params("parallel", "parallel"),
        name="rope_tables",
    )(pos, rows)


def _memkv_body(mem_ref, mn_ref, w_ref, mk_ref, mv_ref):
    mn = _rms(mem_ref[...], mn_ref[...]).astype(BF16)
    kv = _dot(mn, w_ref[...])
    mk = kv[:, :MEM_WIDTH] * (MEM_HEAD_DIM ** -0.5)
    mv = kv[:, MEM_WIDTH:]
    head_of_lane = lax.broadcasted_iota(jnp.int32, (MEM_LEN, MEM_WIDTH), 1) // MEM_HEAD_DIM
    for h in range(MEM_HEADS):
        mk_ref[h] = jnp.where(head_of_lane == h, mk, 0.0).astype(BF16)
        mv_ref[h] = jnp.where(head_of_lane == h, mv, 0.0).astype(BF16)


def _memory_kv(mem, mem_norm, w_kv):
    out = jax.ShapeDtypeStruct((DEPTH, BATCH, MEM_HEADS, MEM_LEN, MEM_WIDTH), BF16)
    spec = pl.BlockSpec((None, None, MEM_HEADS, MEM_LEN, MEM_WIDTH), lambda l, b: (l, b, 0, 0, 0))
    return pl.pallas_call(
        _memkv_body,
        grid=(DEPTH, BATCH),
        in_specs=[pl.BlockSpec((None, MEM_LEN, D_MODEL), lambda l, b: (b, 0, 0)),
                  pl.BlockSpec((1, D_MODEL), lambda l, b: (0, 0)),
                  pl.BlockSpec((None, D_MODEL, 2 * MEM_WIDTH), lambda l, b: (l, 0, 0))],
        out_specs=[spec, spec],
        out_shape=[out, out],
        compiler_params=_params("parallel", "parallel"),
        name="memory_kv",
    )(mem, mem_norm, w_kv)


def _ffn_body(x_ref, gpre_ref, wgu_ref, wd_ref, gpost_ref, o_ref, acc_ref):
    xn = _rms(x_ref[...], gpre_ref[...]).astype(BF16)
    for c in range(D_FF // FF_CHUNK):
        lo = c * FF_CHUNK
        g = _dot(xn, wgu_ref[:, lo:lo + FF_CHUNK])
        u = _dot(xn, wgu_ref[:, D_FF + lo:D_FF + lo + FF_CHUNK])
        a = (_silu(g) * u).astype(BF16)
        d = _dot(a, wd_ref[lo:lo + FF_CHUNK, :])
        if c == 0:
            acc_ref[...] = d
        else:
            acc_ref[...] += d
    o_ref[...] = x_ref[...] + FFN_RES_WEIGHT * _rms(acc_ref[...], gpost_ref[...])


def _ffn(x, g_pre, w_gu, w_down, g_post):
    row = pl.BlockSpec((TM, D_MODEL), lambda i: (i, 0))
    return pl.pallas_call(
        _ffn_body,
        grid=(N_TOK // TM,),
        in_specs=[row, _resident((1, D_MODEL)), _resident((D_MODEL, 2 * D_FF)),
                  _resident((D_FF, D_MODEL)), _resident((1, D_MODEL))],
        out_specs=row,
        out_shape=jax.ShapeDtypeStruct((N_TOK, D_MODEL), F32),
        scratch_shapes=[pltpu.VMEM((TM, D_MODEL), F32)],
        compiler_params=_params("parallel"),
        name="ffn",
    )(x, g_pre, w_gu, w_down, g_post)


def _ret_in_body(x_ref, g_ref, w_ref, cos_ref, sin_ref, q_ref, k_ref, v_ref, gate_ref, qm_ref):
    h = _rms(x_ref[...], g_ref[...]).astype(BF16)
    cos = cos_ref[...]
    sin = sin_ref[...]
    T = TOK_WIDTH
    zq = _dot(h, w_ref[:, 0:T])
    zk = _dot(h, w_ref[:, T:2 * T])
    for hh in range(HEADS):
        sl = slice(hh * HEAD_W, (hh + 1) * HEAD_W)
        t = zq[:, sl]
        q_ref[:, sl] = (t * cos + pltpu.roll(t, HEAD_W // 2, 1) * sin).astype(BF16)
        t = zk[:, sl]
        k_ref[:, sl] = ((t * cos + pltpu.roll(t, HEAD_W // 2, 1) * sin) * (RET_DK ** -0.5)).astype(BF16)
    v_ref[...] = _dot(h, w_ref[:, 2 * T:3 * T]).astype(BF16)
    gate_ref[...] = _dot(h, w_ref[:, 3 * T:4 * T])
    qm_ref[...] = _dot(h, w_ref[:, 4 * T:]).astype(BF16)


def _ret_in(x, g, w_in, cos, sin):
    def tok(w):
        return pl.BlockSpec((TM, w), lambda i: (i, 0))

    def out(w, dt):
        return jax.ShapeDtypeStruct((N_TOK, w), dt)

    return pl.pallas_call(
        _ret_in_body,
        grid=(N_TOK // TM,),
        in_specs=[tok(D_MODEL), _resident((1, D_MODEL)), _resident((D_MODEL, RET_IN)),
                  tok(HEAD_W), tok(HEAD_W)],
        out_specs=[tok(TOK_WIDTH), tok(TOK_WIDTH), tok(TOK_WIDTH), tok(TOK_WIDTH), tok(MEM_WIDTH)],
        out_shape=[out(TOK_WIDTH, BF16), out(TOK_WIDTH, BF16), out(TOK_WIDTH, BF16),
                   out(TOK_WIDTH, F32), out(MEM_WIDTH, BF16)],
        compiler_params=_params("parallel"),
        name="ret_in",
    )(x, g, w_in, cos, sin)


def _retention_body(ld_ref, q_ref, k_ref, v_ref, g_ref, hn_ref, o_ref, st_ref):
    C = RET_CHUNK
    NC = SEQ // C
    hh = pl.program_id(1)
    lgf = ld_ref[0, hh]
    lgb = ld_ref[1, hh]
    row = lax.broadcasted_iota(jnp.int32, (C, HEAD_W), 0).astype(F32)
    kwf = jnp.exp((C - 1.0 - row) * lgf)
    kwb = jnp.exp(row * lgb)
    qwf = jnp.exp((row + 1.0) * lgf)
    qwb = jnp.exp((C - row) * lgb)
    gcf = jnp.exp(jnp.full((HEAD_W, HEAD_W), float(C), F32) * lgf)
    gcb = jnp.exp(jnp.full((HEAD_W, HEAD_W), float(C), F32) * lgb)
    diff = (lax.broadcasted_iota(jnp.int32, (C, C), 0)
            - lax.broadcasted_iota(jnp.int32, (C, C), 1)).astype(F32)
    dec = jnp.where(diff >= 0.0,
                    jnp.exp(jnp.maximum(diff, 0.0) * lgf),
                    jnp.exp(jnp.maximum(-diff, 0.0) * lgb))

    def chunk(n):
        return slice(n * C, (n + 1) * C)

    sf = jnp.zeros((HEAD_W, HEAD_W), F32)
    for n in range(NC):
        st_ref[n, 0:HEAD_W, :] = sf.astype(BF16)
        kn = k_ref[chunk(n), :].astype(F32)
        sf = sf * gcf + _dot_tn((kn * kwf).astype(BF16), v_ref[chunk(n), :])
    sb = jnp.zeros((HEAD_W, HEAD_W), F32)
    for n in reversed(range(NC)):
        st_ref[n, HEAD_W:2 * HEAD_W, :] = sb.astype(BF16)
        kn = k_ref[chunk(n), :].astype(F32)
        sb = sb * gcb + _dot_tn((kn * kwb).astype(BF16), v_ref[chunk(n), :])

    hn = hn_ref[...]
    for n in range(NC):
        qn = q_ref[chunk(n), :]
        qf = qn.astype(F32)
        p = (_dot_nt(qn, k_ref[chunk(n), :]) * dec).astype(BF16)
        qw = jnp.concatenate([(qf * qwf).astype(BF16), (qf * qwb).astype(BF16)], axis=1)
        y = _dot(p, v_ref[chunk(n), :]) + _dot(qw, st_ref[n])
        mu = jnp.mean(y, axis=-1, keepdims=True)
        yc = y - mu
        var = jnp.mean(yc * yc, axis=-1, keepdims=True)
        yn = yc * lax.rsqrt(var + NORM_EPS) * hn
        o_ref[chunk(n), :] = (_silu(g_ref[chunk(n), :]) * yn).astype(BF16)


def _retention(log_decay, q, k, v, gate, head_norm):
    blk = pl.BlockSpec((SEQ, HEAD_W), lambda b, h: (b, h))
    return pl.pallas_call(
        _retention_body,
        grid=(BATCH, HEADS),
        in_specs=[pl.BlockSpec(memory_space=pltpu.SMEM), blk, blk, blk, blk,
                  pl.BlockSpec((None, 1, HEAD_W), lambda b, h: (h, 0, 0))],
        out_specs=blk,
        out_shape=jax.ShapeDtypeStruct((N_TOK, TOK_WIDTH), BF16),
        scratch_shapes=[pltpu.VMEM((SEQ // RET_CHUNK, 2 * HEAD_W, HEAD_W), BF16)],
        compiler_params=_params("parallel", "parallel"),
        name="retention",
    )(log_decay, q, k, v, gate, head_norm)


def _mla_in_body(x_ref, g_ref, w1_ref, qn_ref, kvn_ref, wqa_ref, wqb_ref, wk_ref, wv_ref,
                 cos_ref, sin_ref, q_ref, k_ref, v_ref, qm_ref):
    h = _rms(x_ref[...], g_ref[...]).astype(BF16)
    z = _dot(h, w1_ref[...])
    cq = _rms(z[:, 0:256], qn_ref[...]).astype(BF16)
    ckv = _rms(z[:, 256:384], kvn_ref[...]).astype(BF16)
    cos = cos_ref[...]
    sin = sin_ref[...]
    kr = z[:, 384:512] * cos + z[:, 512:640] * sin
    qa = _dot(cq, wqa_ref[...])
    qb = _dot(cq, wqb_ref[...])
    kn = _dot(ckv, wk_ref[...])
    for hh in range(HEADS):
        sl = slice(hh * HEAD_W, (hh + 1) * HEAD_W)
        q_ref[:, sl] = ((qa[:, sl] * cos + qb[:, sl] * sin) * MLA_SCALE).astype(BF16)
        k_ref[:, sl] = (kn[:, sl] + kr).astype(BF16)
    v_ref[...] = _dot(ckv, wv_ref[...]).astype(BF16)
    qm_ref[...] = z[:, 640:896].astype(BF16)


def _mla_in(x, g, w1, q_norm, kv_norm, wqa, wqb, wk, wv, cos, sin):
    def tok(w):
        return pl.BlockSpec((TM, w), lambda i: (i, 0))

    def out(w):
        return jax.ShapeDtypeStruct((N_TOK, w), BF16)

    return pl.pallas_call(
        _mla_in_body,
        grid=(N_TOK // TM,),
        in_specs=[tok(D_MODEL), _resident((1, D_MODEL)), _resident(w1.shape),
                  _resident((1, MLA_Q_RANK)), _resident((1, MLA_KV_RANK)),
                  _resident(wqa.shape), _resident(wqb.shape), _resident(wk.shape), _resident(wv.shape),
                  tok(HEAD_W), tok(HEAD_W)],
        out_specs=[tok(TOK_WIDTH), tok(TOK_WIDTH), tok(TOK_WIDTH), tok(MEM_WIDTH)],
        out_shape=[out(TOK_WIDTH), out(TOK_WIDTH), out(TOK_WIDTH), out(MEM_WIDTH)],
        compiler_params=_params("parallel"),
        name="mla_in",
    )(x, g, w1, q_norm, kv_norm, wqa, wqb, wk, wv, cos, sin)


def _mla_attn_body(q_ref, k_ref, v_ref, o_ref):
    s = _dot_nt(q_ref[...], k_ref[...])
    p = jnp.exp(s - jnp.max(s, axis=-1, keepdims=True))
    l = jnp.sum(p, axis=-1, keepdims=True)
    o = _dot(p.astype(BF16), v_ref[...])
    o_ref[...] = (o * (1.0 / l)).astype(BF16)


def _mla_attn(q, k, v):
    nq = SEQ // TQ
    kv = pl.BlockSpec((SEQ, HEAD_W), lambda b, h, i: (b, h))
    qo = pl.BlockSpec((TQ, HEAD_W), lambda b, h, i: (b * nq + i, h))
    return pl.pallas_call(
        _mla_attn_body,
        grid=(BATCH, HEADS, nq),
        in_specs=[qo, kv, kv],
        out_specs=qo,
        out_shape=jax.ShapeDtypeStruct((N_TOK, TOK_WIDTH), BF16),
        compiler_params=_params("parallel", "parallel", "parallel"),
        name="mla_attn",
    )(q, k, v)


def _out_body(x_ref, tok_ref, qm_ref, mk_ref, mv_ref, wo_ref, g_ref, o_ref):
    qm = qm_ref[...]
    mem = jnp.zeros((TM, MEM_WIDTH), F32)
    for h in range(MEM_HEADS):
        s = _dot_nt(qm, mk_ref[h])
        p = jnp.exp(s - jnp.max(s, axis=-1, keepdims=True))
        l = jnp.sum(p, axis=-1, keepdims=True)
        mem = mem + _dot((p * (1.0 / l)).astype(BF16), mv_ref[h])
    mix = _dot(tok_ref[...], wo_ref[0:TOK_WIDTH, :]) + _dot(mem.astype(BF16), wo_ref[TOK_WIDTH:, :])
    o_ref[...] = x_ref[...] + _rms(mix, g_ref[...])


def _out_proj(layer, x, tok, qm, mk, mv, w_o, g):
    tiles_per_batch = SEQ // TM
    mem_spec = pl.BlockSpec((None, None, MEM_HEADS, MEM_LEN, MEM_WIDTH),
                            lambda i: (layer, i // tiles_per_batch, 0, 0, 0))

    def tokspec(w):
        return pl.BlockSpec((TM, w), lambda i: (i, 0))

    return pl.pallas_call(
        _out_body,
        grid=(N_TOK // TM,),
        in_specs=[tokspec(D_MODEL), tokspec(TOK_WIDTH), tokspec(MEM_WIDTH), mem_spec, mem_spec,
                  _resident((D_MODEL, D_MODEL)), _resident((1, D_MODEL))],
        out_specs=tokspec(D_MODEL),
        out_shape=jax.ShapeDtypeStruct((N_TOK, D_MODEL), F32),
        compiler_params=_params("parallel"),
        name="out_proj",
    )(x, tok, qm, mk, mv, w_o, g)


def _pad_heads(w, width, offset=0):
    k = w.shape[0]
    out = jnp.zeros((k, HEADS, HEAD_W), w.dtype)
    out = out.at[:, :, offset:offset + width].set(w)
    return out.reshape(k, HEADS * HEAD_W)


def _rot_half_cols(w):
    half = w.shape[-1] // 2
    return jnp.concatenate([-w[..., half:], w[..., :half]], axis=-1)


def _rope_rows():
    def inv(dim):
        return 1.0 / (ROPE_THETA ** (jnp.arange(0, dim, 2, dtype=F32) / dim))

    z32 = jnp.zeros((HEAD_W - MLA_NOPE - MLA_ROPE,), F32)
    z64 = jnp.zeros((MLA_NOPE,), F32)
    o64 = jnp.ones((MLA_NOPE,), F32)
    o32 = jnp.ones((MLA_ROPE,), F32)
    ir = inv(RET_DK)
    im = inv(MLA_ROPE)
    ret = jnp.stack([jnp.concatenate([ir, ir]), jnp.ones((HEAD_W,), F32),
                     jnp.concatenate([-o64, o64])])
    mla = jnp.stack([jnp.concatenate([z64, im, im, z32]), jnp.concatenate([o64, o32, z32]),
                     jnp.concatenate([z64, o32, z32])])
    return jnp.stack([ret, mla])


def kernel(x, mem, positions, norm_gains, ffn_w_gu, ffn_w_down, w_o, mem_norm, mem_w_kv,
           ret_w_in, ret_log_decay, ret_head_norm,
           mla_w_in, mla_q_norm, mla_kv_norm, mla_w_uq, mla_w_ukv):
    xs = x.reshape(N_TOK, D_MODEL)
    pos = positions.astype(F32).reshape(N_TOK, 1)
    cos_t, sin_t = _rope_tables(pos, _rope_rows())
    mk, mv = _memory_kv(mem, mem_norm.reshape(1, D_MODEL), mem_w_kv.astype(BF16))

    gains = norm_gains.reshape(DEPTH, 6, 1, D_MODEL)
    w_gu = ffn_w_gu.astype(BF16)
    w_dn = ffn_w_down.astype(BF16)
    w_ob = w_o.astype(BF16)

    for l in range(DEPTH):
        r = l // 2
        xs = _ffn(xs, gains[l, 0], w_gu[l, 0], w_dn[l, 0], gains[l, 1])
        if l % 2 == 0:
            q, k, v, gate, qm = _ret_in(xs, gains[l, 2], ret_w_in[r].astype(BF16), cos_t[0], sin_t[0])
            tok = _retention(ret_log_decay[r], q, k, v, gate, ret_head_norm[r].reshape(HEADS, 1, HEAD_W))
        else:
            w_in = mla_w_in[r]
            c0 = MLA_Q_RANK + MLA_KV_RANK
            w_kr = w_in[:, c0:c0 + MLA_ROPE]
            pad_l = jnp.zeros((D_MODEL, MLA_NOPE), F32)
            pad_r = jnp.zeros((D_MODEL, HEAD_W - MLA_NOPE - MLA_ROPE), F32)
            w1 = jnp.concatenate([w_in[:, :c0], pad_l, w_kr, pad_r, pad_l, _rot_half_cols(w_kr), pad_r,
                                  w_in[:, c0 + MLA_ROPE:]], axis=1).astype(BF16)
            uq = mla_w_uq[r].reshape(MLA_Q_RANK, HEADS, MLA_NOPE + MLA_ROPE)
            wqa = _pad_heads(uq, MLA_NOPE + MLA_ROPE).astype(BF16)
            wqb = _pad_heads(_rot_half_cols(uq[:, :, MLA_NOPE:]), MLA_ROPE, MLA_NOPE).astype(BF16)
            ukv = mla_w_ukv[r].reshape(MLA_KV_RANK, HEADS, MLA_NOPE + HEAD_W)
            wk = _pad_heads(ukv[:, :, :MLA_NOPE], MLA_NOPE).astype(BF16)
            wv = ukv[:, :, MLA_NOPE:].reshape(MLA_KV_RANK, TOK_WIDTH).astype(BF16)
            q, k, v, qm = _mla_in(xs, gains[l, 2], w1, mla_q_norm[r].reshape(1, -1),
                                  mla_kv_norm[r].reshape(1, -1), wqa, wqb, wk, wv, cos_t[1], sin_t[1])
            tok = _mla_attn(q, k, v)
        xs = _out_proj(l, xs, tok, qm, mk, mv, w_ob[l], gains[l, 3])
        xs = _ffn(xs, gains[l, 4], w_gu[l, 1], w_dn[l, 1], gains[l, 5])
    return xs.reshape(BATCH, SEQ, D_MODEL)
```

```python
import math

import jax
import jax.numpy as jnp
from jax import lax
from jax.experimental import pallas as pl
from jax.experimental.pallas import tpu as pltpu

D_MODEL = 1024
BATCH = 8
SEQ = 2048
DEPTH = 4
N_TOK = BATCH * SEQ
MEM_LEN = 256
MEM_HEADS = 4
MEM_HEAD_DIM = 64
MEM_WIDTH = MEM_HEADS * MEM_HEAD_DIM
TOK_WIDTH = D_MODEL - MEM_WIDTH
HEADS = 6
HEAD_W = 128
RET_DK = 128
RET_IN = 4 * TOK_WIDTH + MEM_WIDTH
MLA_NOPE = 64
MLA_ROPE = 32
MLA_Q_RANK = 256
MLA_KV_RANK = 128
LOG2E = math.log2(math.e)
MLA_SCALE = (MLA_NOPE + MLA_ROPE) ** -0.5 * LOG2E
MEM_SCALE = MEM_HEAD_DIM ** -0.5 * LOG2E
D_FF = 2816
FFN_RES_WEIGHT = 0.5
ROPE_THETA = 10000.0
NORM_EPS = 1e-6

V7X_VMEM_BYTES = 64 * 1024 * 1024
VMEM_LIMIT = V7X_VMEM_BYTES - 8 * 1024 * 1024
MXU_TILE = 256
TM = 512
FF_CHUNK = MXU_TILE
RET_CHUNK = MXU_TILE
ATT_QC = MXU_TILE

F32 = jnp.float32
BF16 = jnp.bfloat16


def _dot(a, b):
    return jnp.dot(a, b, preferred_element_type=F32)


def _dot_nt(a, b):
    return lax.dot_general(a, b, (((1,), (1,)), ((), ())), preferred_element_type=F32)


def _dot_tn(a, b):
    return lax.dot_general(a, b, (((0,), (0,)), ((), ())), preferred_element_type=F32)


def _rms(x, g):
    ms = jnp.mean(x * x, axis=-1, keepdims=True)
    return x * lax.rsqrt(ms + NORM_EPS) * g


def _silu(g):
    return g * jax.nn.sigmoid(g)


def _softmax_cols(s):
    p = jnp.exp2(s - jnp.max(s, axis=0, keepdims=True))
    return p, 1.0 / jnp.sum(p, axis=0, keepdims=True)


def _params(*semantics):
    return pltpu.CompilerParams(dimension_semantics=semantics, vmem_limit_bytes=VMEM_LIMIT)


def _resident(shape, *lead):
    n = len(shape) - len(lead)
    return pl.BlockSpec((None,) * len(lead) + tuple(shape[len(lead):]),
                        lambda *_: tuple(lead) + (0,) * n, pipeline_mode=pl.Buffered(1))


def _tok(width):
    return pl.BlockSpec((TM, width), lambda i: (i, 0))


def _rope_body(pos_ref, inv_ref, cos_ref, sin_ref):
    ang = pos_ref[...] * inv_ref[...]
    c = jnp.cos(ang)
    s = jnp.sin(ang)
    lane = lax.broadcasted_iota(jnp.int32, (1, HEAD_W), 1)
    half = RET_DK // 2
    cos_ref[0] = jnp.where(lane < half, c, pltpu.roll(c, half, 1))
    sin_ref[0] = jnp.where(lane < half, -s, pltpu.roll(s, half, 1))
    hm = MLA_ROPE // 2
    c2 = jnp.where(lane < MLA_NOPE + hm, c, pltpu.roll(c, hm, 1))
    s2 = jnp.where(lane < MLA_NOPE + hm, s, pltpu.roll(s, hm, 1))
    cos_ref[1] = jnp.where(lane < MLA_NOPE, 1.0, jnp.where(lane < MLA_NOPE + MLA_ROPE, c2, 0.0))
    sin_ref[1] = jnp.where(lane < MLA_NOPE, 0.0, jnp.where(lane < MLA_NOPE + MLA_ROPE, s2, 0.0))


def _rope_tables(pos, inv_row):
    tm = 2048
    out = pl.BlockSpec((2, tm, HEAD_W), lambda i: (0, i, 0))
    return pl.pallas_call(
        _rope_body,
        grid=(N_TOK // tm,),
        in_specs=[pl.BlockSpec((tm, 1), lambda i: (i, 0)), pl.BlockSpec((1, HEAD_W), lambda i: (0, 0))],
        out_specs=[out, out],
        out_shape=[jax.ShapeDtypeStruct((2, N_TOK, HEAD_W), F32)] * 2,
        compiler_params=_params("parallel"),
        name="rope_tables",
    )(pos, inv_row)


def _memkv_body(mem_ref, mn_ref, w_ref, mk_ref, mvt_ref):
    mn = _rms(mem_ref[...], mn_ref[...]).astype(BF16)
    kv = _dot(mn, w_ref[...])
    mk = kv[:, :MEM_WIDTH] * MEM_SCALE
    head_of_lane = lax.broadcasted_iota(jnp.int32, (MEM_LEN, MEM_WIDTH), 1) // MEM_HEAD_DIM
    for h in range(MEM_HEADS):
        mk_ref[h * MEM_LEN:(h + 1) * MEM_LEN, :] = jnp.where(head_of_lane == h, mk, 0.0).astype(BF16)
    mvt_ref[...] = kv[:, MEM_WIDTH:].T.astype(BF16)


def _memory_kv(mem, mem_norm, w_kv):
    return pl.pallas_call(
        _memkv_body,
        grid=(DEPTH, BATCH),
        in_specs=[pl.BlockSpec((None, MEM_LEN, D_MODEL), lambda l, b: (b, 0, 0)),
                  pl.BlockSpec((1, D_MODEL), lambda l, b: (0, 0)),
                  pl.BlockSpec((None, D_MODEL, 2 * MEM_WIDTH), lambda l, b: (l, 0, 0))],
        out_specs=[pl.BlockSpec((None, None, MEM_HEADS * MEM_LEN, MEM_WIDTH), lambda l, b: (l, b, 0, 0)),
                   pl.BlockSpec((None, None, MEM_WIDTH, MEM_LEN), lambda l, b: (l, b, 0, 0))],
        out_shape=[jax.ShapeDtypeStruct((DEPTH, BATCH, MEM_HEADS * MEM_LEN, MEM_WIDTH), BF16),
                   jax.ShapeDtypeStruct((DEPTH, BATCH, MEM_WIDTH, MEM_LEN), BF16)],
        compiler_params=_params("parallel", "parallel"),
        name="memory_kv",
    )(mem, mem_norm, w_kv)


def _ffn_body(x_ref, gpre_ref, wgu_ref, wd_ref, gpost_ref, o_ref, acc_ref):
    xn = _rms(x_ref[...], gpre_ref[...]).astype(BF16)
    for c in range(D_FF // FF_CHUNK):
        lo = c * FF_CHUNK
        g = _dot(xn, wgu_ref[:, lo:lo + FF_CHUNK])
        u = _dot(xn, wgu_ref[:, D_FF + lo:D_FF + lo + FF_CHUNK])
        a = (_silu(g) * u).astype(BF16)
        d = _dot(a, wd_ref[lo:lo + FF_CHUNK, :])
        if c == 0:
            acc_ref[...] = d
        else:
            acc_ref[...] += d
    o_ref[...] = x_ref[...] + FFN_RES_WEIGHT * _rms(acc_ref[...], gpost_ref[...])


def _ffn(layer, half, x, gains, w_gu, w_down):
    return pl.pallas_call(
        _ffn_body,
        grid=(N_TOK // TM,),
        in_specs=[_tok(D_MODEL), _resident(gains.shape, layer, 4 * half),
                  _resident(w_gu.shape, layer, half), _resident(w_down.shape, layer, half),
                  _resident(gains.shape, layer, 4 * half + 1)],
        out_specs=_tok(D_MODEL),
        out_shape=jax.ShapeDtypeStruct((N_TOK, D_MODEL), F32),
        scratch_shapes=[pltpu.VMEM((TM, D_MODEL), F32)],
        compiler_params=_params("parallel"),
        name="ffn",
    )(x, gains, w_gu, w_down, gains)


def _ret_in_body(x_ref, g_ref, w_ref, cos_ref, sin_ref, q_ref, k_ref, v_ref, gate_ref, qm_ref):
    h = _rms(x_ref[...], g_ref[...]).astype(BF16)
    cos = cos_ref[...]
    sin = sin_ref[...]
    T = TOK_WIDTH
    zq = _dot(h, w_ref[:, 0:T])
    zk = _dot(h, w_ref[:, T:2 * T])
    for hh in range(HEADS):
        sl = slice(hh * HEAD_W, (hh + 1) * HEAD_W)
        t = zq[:, sl]
        q_ref[:, sl] = (t * cos + pltpu.roll(t, HEAD_W // 2, 1) * sin).astype(BF16)
        t = zk[:, sl]
        k_ref[:, sl] = ((t * cos + pltpu.roll(t, HEAD_W // 2, 1) * sin) * (RET_DK ** -0.5)).astype(BF16)
    v_ref[...] = _dot(h, w_ref[:, 2 * T:3 * T]).astype(BF16)
    gate_ref[...] = _dot(h, w_ref[:, 3 * T:4 * T])
    qm_ref[...] = _dot(h, w_ref[:, 4 * T:]).astype(BF16)


def _ret_in(layer, x, gains, w_in, cos, sin):
    def out(w, dt):
        return jax.ShapeDtypeStruct((N_TOK, w), dt)

    table = pl.BlockSpec((None, TM, HEAD_W), lambda i: (0, i, 0))
    return pl.pallas_call(
        _ret_in_body,
        grid=(N_TOK // TM,),
        in_specs=[_tok(D_MODEL), _resident(gains.shape, layer, 2), _resident(w_in.shape, layer // 2),
                  table, table],
        out_specs=[_tok(TOK_WIDTH), _tok(TOK_WIDTH), _tok(TOK_WIDTH), _tok(TOK_WIDTH), _tok(MEM_WIDTH)],
        out_shape=[out(TOK_WIDTH, BF16), out(TOK_WIDTH, BF16), out(TOK_WIDTH, BF16),
                   out(TOK_WIDTH, F32), out(MEM_WIDTH, BF16)],
        compiler_params=_params("parallel"),
        name="ret_in",
    )(x, gains, w_in, cos, sin)


def _retention_body(ld_ref, q_ref, k_ref, v_ref, g_ref, hn_ref, o_ref, st_ref):
    C = RET_CHUNK
    NC = SEQ // C
    hh = pl.program_id(1)
    lgf = ld_ref[0, hh]
    lgb = ld_ref[1, hh]
    row = lax.broadcasted_iota(jnp.int32, (C, HEAD_W), 0).astype(F32)
    kwf = jnp.exp((C - 1.0 - row) * lgf)
    kwb = jnp.exp(row * lgb)
    qwf = jnp.exp((row + 1.0) * lgf)
    qwb = jnp.exp((C - row) * lgb)
    gcf = jnp.exp(jnp.full((HEAD_W, HEAD_W), float(C), F32) * lgf)
    gcb = jnp.exp(jnp.full((HEAD_W, HEAD_W), float(C), F32) * lgb)
    diff = (lax.broadcasted_iota(jnp.int32, (C, C), 0)
            - lax.broadcasted_iota(jnp.int32, (C, C), 1)).astype(F32)
    dec = jnp.where(diff >= 0.0,
                    jnp.exp(jnp.maximum(diff, 0.0) * lgf),
                    jnp.exp(jnp.maximum(-diff, 0.0) * lgb))

    def chunk(n):
        return slice(n * C, (n + 1) * C)

    sf = jnp.zeros((HEAD_W, HEAD_W), F32)
    for n in range(NC):
        st_ref[n, 0:HEAD_W, :] = sf.astype(BF16)
        kn = k_ref[chunk(n), :].astype(F32)
        sf = sf * gcf + _dot_tn((kn * kwf).astype(BF16), v_ref[chunk(n), :])
    sb = jnp.zeros((HEAD_W, HEAD_W), F32)
    for n in reversed(range(NC)):
        st_ref[n, HEAD_W:2 * HEAD_W, :] = sb.astype(BF16)
        kn = k_ref[chunk(n), :].astype(F32)
        sb = sb * gcb + _dot_tn((kn * kwb).astype(BF16), v_ref[chunk(n), :])

    hn = hn_ref[...]
    for n in range(NC):
        qn = q_ref[chunk(n), :]
        qf = qn.astype(F32)
        p = (_dot_nt(qn, k_ref[chunk(n), :]) * dec).astype(BF16)
        qw = jnp.concatenate([(qf * qwf).astype(BF16), (qf * qwb).astype(BF16)], axis=1)
        y = _dot(p, v_ref[chunk(n), :]) + _dot(qw, st_ref[n])
        mu = jnp.mean(y, axis=-1, keepdims=True)
        yc = y - mu
        var = jnp.mean(yc * yc, axis=-1, keepdims=True)
        yn = yc * lax.rsqrt(var + NORM_EPS) * hn
        o_ref[chunk(n), :] = (_silu(g_ref[chunk(n), :]) * yn).astype(BF16)


def _retention(layer, log_decay, q, k, v, gate, head_norm):
    blk = pl.BlockSpec((SEQ, HEAD_W), lambda b, h: (b, h))
    r = layer // 2
    return pl.pallas_call(
        _retention_body,
        grid=(BATCH, HEADS),
        in_specs=[pl.BlockSpec(memory_space=pltpu.SMEM), blk, blk, blk, blk,
                  pl.BlockSpec((None, None, 1, HEAD_W), lambda b, h: (r, h, 0, 0))],
        out_specs=blk,
        out_shape=jax.ShapeDtypeStruct((N_TOK, TOK_WIDTH), BF16),
        scratch_shapes=[pltpu.VMEM((SEQ // RET_CHUNK, 2 * HEAD_W, HEAD_W), BF16)],
        compiler_params=_params("parallel", "parallel"),
        name="retention",
    )(log_decay[r], q, k, v, gate, head_norm)


def _mla_in_body(x_ref, g_ref, w1_ref, qn_ref, kvn_ref, wqa_ref, wqb_ref, wk_ref, wvt_ref,
                 cos_ref, sin_ref, q_ref, k_ref, vt_ref, qm_ref):
    h = _rms(x_ref[...], g_ref[...]).astype(BF16)
    z = _dot(h, w1_ref[...])
    cq = _rms(z[:, 0:256], qn_ref[...]).astype(BF16)
    ckv = _rms(z[:, 256:384], kvn_ref[...]).astype(BF16)
    cos = cos_ref[...]
    sin = sin_ref[...]
    kr = z[:, 384:512] * cos + z[:, 512:640] * sin
    qa = _dot(cq, wqa_ref[...])
    qb = _dot(cq, wqb_ref[...])
    kn = _dot(ckv, wk_ref[...])
    for hh in range(HEADS):
        sl = slice(hh * HEAD_W, (hh + 1) * HEAD_W)
        q_ref[:, sl] = ((qa[:, sl] * cos + qb[:, sl] * sin) * MLA_SCALE).astype(BF16)
        k_ref[:, sl] = (kn[:, sl] + kr).astype(BF16)
    vt_ref[...] = _dot_nt(wvt_ref[...], ckv).astype(BF16)
    qm_ref[...] = z[:, 640:896].astype(BF16)


def _mla_in(layer, x, gains, w1, wqa, wqb, wk, wvt, q_norm, kv_norm, cos, sin):
    def out(w):
        return jax.ShapeDtypeStruct((N_TOK, w), BF16)

    r = layer // 2
    table = pl.BlockSpec((None, TM, HEAD_W), lambda i: (1, i, 0))
    return pl.pallas_call(
        _mla_in_body,
        grid=(N_TOK // TM,),
        in_specs=[_tok(D_MODEL), _resident(gains.shape, layer, 2), _resident(w1.shape),
                  _resident(q_norm.shape, r), _resident(kv_norm.shape, r),
                  _resident(wqa.shape), _resident(wqb.shape), _resident(wk.shape), _resident(wvt.shape),
                  table, table],
        out_specs=[_tok(TOK_WIDTH), _tok(TOK_WIDTH), pl.BlockSpec((TOK_WIDTH, TM), lambda i: (0, i)),
                   _tok(MEM_WIDTH)],
        out_shape=[out(TOK_WIDTH), out(TOK_WIDTH), jax.ShapeDtypeStruct((TOK_WIDTH, N_TOK), BF16),
                   out(MEM_WIDTH)],
        compiler_params=_params("parallel"),
        name="mla_in",
    )(x, gains, w1, q_norm, kv_norm, wqa, wqb, wk, wvt, cos, sin)


def _mla_attn_body(q_ref, k_ref, vt_ref, o_ref, s_ref, p_ref):
    nc = SEQ // ATT_QC

    def block(c):
        return slice(c * ATT_QC, (c + 1) * ATT_QC)

    s_ref[0] = _dot_nt(k_ref[...], q_ref[block(0), :])
    for c in range(nc):
        if c + 1 < nc:
            s_ref[(c + 1) % 2] = _dot_nt(k_ref[...], q_ref[block(c + 1), :])
        p, inv_l = _softmax_cols(s_ref[c % 2])
        p_ref[c % 2] = p.astype(BF16)
        ot = _dot(vt_ref[...], p_ref[c % 2]) * inv_l
        o_ref[block(c), :] = ot.T.astype(BF16)


def _mla_attn(q, k, vt):
    blk = pl.BlockSpec((SEQ, HEAD_W), lambda b, h: (b, h))
    return pl.pallas_call(
        _mla_attn_body,
        grid=(BATCH, HEADS),
        in_specs=[blk, blk, pl.BlockSpec((HEAD_W, SEQ), lambda b, h: (h, b))],
        out_specs=blk,
        out_shape=jax.ShapeDtypeStruct((N_TOK, TOK_WIDTH), BF16),
        scratch_shapes=[pltpu.VMEM((2, SEQ, ATT_QC), F32), pltpu.VMEM((2, SEQ, ATT_QC), BF16)],
        compiler_params=_params("parallel", "parallel"),
        name="mla_attn",
    )(q, k, vt)


def _out_body(x_ref, tok_ref, qm_ref, mk_ref, mvt_ref, wo_ref, g_ref, o_ref):
    s = _dot_nt(mk_ref[...], qm_ref[...])
    outs = []
    for h in range(MEM_HEADS):
        p, inv_l = _softmax_cols(s[h * MEM_LEN:(h + 1) * MEM_LEN, :])
        outs.append(_dot(mvt_ref[h * MEM_HEAD_DIM:(h + 1) * MEM_HEAD_DIM, :], (p * inv_l).astype(BF16)))
    mem = jnp.concatenate(outs, axis=0).T.astype(BF16)
    mix = _dot(tok_ref[...], wo_ref[0:TOK_WIDTH, :]) + _dot(mem, wo_ref[TOK_WIDTH:, :])
    o_ref[...] = x_ref[...] + _rms(mix, g_ref[...])


def _out_proj(layer, x, tok, qm, mk, mvt, w_o, gains):
    tiles_per_batch = SEQ // TM

    def mem_spec(rows, cols):
        return pl.BlockSpec((None, None, rows, cols), lambda i: (layer, i // tiles_per_batch, 0, 0))

    return pl.pallas_call(
        _out_body,
        grid=(N_TOK // TM,),
        in_specs=[_tok(D_MODEL), _tok(TOK_WIDTH), _tok(MEM_WIDTH),
                  mem_spec(MEM_HEADS * MEM_LEN, MEM_WIDTH), mem_spec(MEM_WIDTH, MEM_LEN),
                  _resident(w_o.shape, layer), _resident(gains.shape, layer, 3)],
        out_specs=_tok(D_MODEL),
        out_shape=jax.ShapeDtypeStruct((N_TOK, D_MODEL), F32),
        compiler_params=_params("parallel"),
        name="out_proj",
    )(x, tok, qm, mk, mvt, w_o, gains)


def _pad_heads(w, width, offset=0):
    k = w.shape[0]
    out = jnp.zeros((k, HEADS, HEAD_W), w.dtype)
    out = out.at[:, :, offset:offset + width].set(w)
    return out.reshape(k, HEADS * HEAD_W)


def _rot_half_cols(w):
    half = w.shape[-1] // 2
    return jnp.concatenate([-w[..., half:], w[..., :half]], axis=-1)


def _rope_inv_row():
    def inv(dim):
        return 1.0 / (ROPE_THETA ** (jnp.arange(0, dim, 2, dtype=F32) / dim))

    pad = jnp.zeros((HEAD_W - RET_DK // 2 - MLA_ROPE // 2,), F32)
    return jnp.concatenate([inv(RET_DK), inv(MLA_ROPE), pad]).reshape(1, HEAD_W)


def _mla_weights(w_in, w_uq, w_ukv):
    c0 = MLA_Q_RANK + MLA_KV_RANK
    w_kr = w_in[:, c0:c0 + MLA_ROPE]
    pad_l = jnp.zeros((D_MODEL, MLA_NOPE), F32)
    pad_r = jnp.zeros((D_MODEL, HEAD_W - MLA_NOPE - MLA_ROPE), F32)
    w1 = jnp.concatenate([w_in[:, :c0], pad_l, w_kr, pad_r, pad_l, _rot_half_cols(w_kr), pad_r,
                          w_in[:, c0 + MLA_ROPE:]], axis=1)
    uq = w_uq.reshape(MLA_Q_RANK, HEADS, MLA_NOPE + MLA_ROPE)
    wqa = _pad_heads(uq, MLA_NOPE + MLA_ROPE)
    wqb = _pad_heads(_rot_half_cols(uq[:, :, MLA_NOPE:]), MLA_ROPE, MLA_NOPE)
    ukv = w_ukv.reshape(MLA_KV_RANK, HEADS, MLA_NOPE + HEAD_W)
    wk = _pad_heads(ukv[:, :, :MLA_NOPE], MLA_NOPE)
    wvt = ukv[:, :, MLA_NOPE:].reshape(MLA_KV_RANK, TOK_WIDTH).T
    return [w.astype(BF16) for w in (w1, wqa, wqb, wk, wvt)]


def kernel(x, mem, positions, norm_gains, ffn_w_gu, ffn_w_down, w_o, mem_norm, mem_w_kv,
           ret_w_in, ret_log_decay, ret_head_norm,
           mla_w_in, mla_q_norm, mla_kv_norm, mla_w_uq, mla_w_ukv):
    xs = x.reshape(N_TOK, D_MODEL)
    pos = positions.astype(F32).reshape(N_TOK, 1)
    cos_t, sin_t = _rope_tables(pos, _rope_inv_row())
    mk, mvt = _memory_kv(mem, mem_norm.reshape(1, D_MODEL), mem_w_kv.astype(BF16))

    gains = norm_gains.reshape(DEPTH, 6, 1, D_MODEL)
    w_gu = ffn_w_gu.astype(BF16)
    w_dn = ffn_w_down.astype(BF16)
    w_ob = w_o.astype(BF16)
    w_ret = ret_w_in.astype(BF16)
    head_norm = ret_head_norm.reshape(-1, HEADS, 1, HEAD_W)
    q_norm = mla_q_norm.reshape(-1, 1, MLA_Q_RANK)
    kv_norm = mla_kv_norm.reshape(-1, 1, MLA_KV_RANK)

    for l in range(DEPTH):
        r = l // 2
        xs = _ffn(l, 0, xs, gains, w_gu, w_dn)
        if l % 2 == 0:
            q, k, v, gate, qm = _ret_in(l, xs, gains, w_ret, cos_t, sin_t)
            tok = _retention(l, ret_log_decay, q, k, v, gate, head_norm)
        else:
            q, k, vt, qm = _mla_in(l, xs, gains, *_mla_weights(mla_w_in[r], mla_w_uq[r], mla_w_ukv[r]),
                                   q_norm, kv_norm, cos_t, sin_t)
            tok = _mla_attn(q, k, vt)
        xs = _out_proj(l, xs, tok, qm, mk, mvt, w_ob, gains)
        xs = _ffn(l, 1, xs, gains, w_gu, w_dn)
    return xs.reshape(BATCH, SEQ, D_MODEL)
```
